```python
import math
import jax, jax.numpy as jnp
from jax import lax
import numpy as np

D_MODEL = 1024
BATCH = 16
SEQ = 2048
DEPTH = 1

MIX_WIDTH = 2 * D_MODEL
D_SSD = MIX_WIDTH // 2
D_S5 = MIX_WIDTH - D_SSD
SSD_HEAD_DIM = 64
SSD_HEADS = D_SSD // SSD_HEAD_DIM
SSD_GROUPS = 2
SSD_HEADS_PER_GROUP = SSD_HEADS // SSD_GROUPS
SSD_STATE = 128
SSD_CONV = 5
SSD_CHUNK = 128
D_XBC = D_SSD + 2 * SSD_GROUPS * SSD_STATE
S5_GROUP_CH = 16
S5_GROUPS = D_S5 // S5_GROUP_CH
S5_STATE = 64
D_FF = ((8 * D_MODEL + 3 * 256 - 1) // (3 * 256)) * 256
D_IN_PROJ = D_SSD + D_XBC + 2 * SSD_HEADS + D_S5
RMS_EPS = 1e-6
GATED_NORM_EPS = 1e-5
S5_MAX_REAL = -1e-4

kernel_name = "hymba_ssd_s5_sandwich_block"


def _rms_norm(x, w, eps=RMS_EPS):
    xf = x.astype(jnp.float32)
    y = xf * lax.rsqrt(jnp.mean(xf * xf, axis=-1, keepdims=True) + eps)
    return (y * w.astype(jnp.float32)).astype(x.dtype)


def _depthwise_conv_centred(x, w, b):
    ch = x.shape[-1]
    pad = SSD_CONV // 2
    y = lax.conv_general_dilated(
        x, w[:, None, :].astype(x.dtype), window_strides=(1,),
        padding=[(pad, pad)], dimension_numbers=("NWC", "WIO", "NWC"),
        feature_group_count=ch)
    return y + b.astype(x.dtype)


def _ssd_scan(xh, dt, a_head, b_ssm, c_ssm):
    bsz, seq = xh.shape[0], xh.shape[1]
    nc = seq // SSD_CHUNK
    g, r, l = SSD_GROUPS, SSD_HEADS_PER_GROUP, SSD_CHUNK
    xdt = (xh * dt[..., None]).reshape(bsz, nc, l, g, r, SSD_HEAD_DIM)
    a = (dt.astype(jnp.float32) * a_head.astype(jnp.float32))
    a = a.reshape(bsz, nc, l, g, r).transpose(0, 1, 3, 4, 2)
    bc = b_ssm.reshape(bsz, nc, l, g, SSD_STATE)
    cc = c_ssm.reshape(bsz, nc, l, g, SSD_STATE)
    a_cum = jnp.cumsum(a, axis=-1)
    lower = jnp.tril(jnp.ones((l, l), dtype=bool))
    decay = jnp.exp(jnp.where(lower, a_cum[..., :, None] - a_cum[..., None, :], -jnp.inf))
    scores = jnp.einsum("bclgn,bcsgn->bcgls", cc, bc)
    y_diag = jnp.einsum("bcgrls,bcsgrp->bclgrp", scores[:, :, :, None] * decay, xdt)
    decay_to_end = jnp.exp(a_cum[..., -1:] - a_cum)
    states = jnp.einsum("bclgn,bcgrl,bclgrp->bcgrpn", bc, decay_to_end, xdt)
    a_last = a_cum[..., -1]
    t_incl = jnp.cumsum(a_last, axis=1)
    t_excl = t_incl - a_last
    strict = jnp.tril(jnp.ones((nc, nc), dtype=bool), k=-1)[None, :, :, None, None]
    m = jnp.exp(jnp.where(strict, t_excl[:, :, None] - t_incl[:, None, :], -jnp.inf))
    states_in = jnp.einsum("bzcgr,bcgrpn->bzgrpn", m, states)
    y_off = jnp.einsum("bclgn,bcgrpn,bcgrl->bclgrp", cc, states_in, jnp.exp(a_cum))
    return (y_diag + y_off).reshape(bsz, seq, SSD_HEADS, SSD_HEAD_DIM)


def _ssd_mixer(z, xbc, dt_raw, conv_w, conv_b, dt_bias, a_log, d_skip, norm_w):
    bsz, seq = z.shape[0], z.shape[1]
    xbc = jax.nn.silu(_depthwise_conv_centred(xbc, conv_w, conv_b))
    xs, b_ssm, c_ssm = jnp.split(xbc, [D_SSD, D_SSD + SSD_GROUPS * SSD_STATE], axis=-1)
    xh = xs.reshape(bsz, seq, SSD_HEADS, SSD_HEAD_DIM)
    b_ssm = b_ssm.reshape(bsz, seq, SSD_GROUPS, SSD_STATE)
    c_ssm = c_ssm.reshape(bsz, seq, SSD_GROUPS, SSD_STATE)
    dt = jax.nn.softplus(dt_raw.reshape(bsz, seq, 2, SSD_HEADS).astype(jnp.float32) + dt_bias)
    a = -jnp.exp(a_log.astype(jnp.float32))
    y_f = _ssd_scan(xh, dt[:, :, 0], a[0], b_ssm, c_ssm)
    y_b = jnp.flip(_ssd_scan(jnp.flip(xh, 1), jnp.flip(dt[:, :, 1], 1), a[1],
                             jnp.flip(b_ssm, 1), jnp.flip(c_ssm, 1)), 1)
    y = y_f + y_b + xh * d_skip[:, None]
    y = y.reshape(bsz, seq, D_SSD) * jax.nn.silu(z)
    y = _rms_norm(y.reshape(bsz, seq, SSD_GROUPS, D_SSD // SSD_GROUPS),
                  norm_w.reshape(SSD_GROUPS, D_SSD // SSD_GROUPS), GATED_NORM_EPS)
    return y.reshape(bsz, seq, D_SSD)


def _complex_linear_combine(e1, e2):
    ar1, ai1, br1, bi1 = e1
    ar2, ai2, br2, bi2 = e2
    ar = ar2 * ar1 - ai2 * ai1
    ai = ar2 * ai1 + ai2 * ar1
    br = ar2 * br1 - ai2 * bi1 + br2
    bi = ar2 * bi1 + ai2 * br1 + bi2
    return (ar, ai, br, bi)


def _s5_scan(ug, lam_re, lam_im, log_dt, b_re, b_im, c_re, c_im):
    seq = ug.shape[1]
    lr = jnp.minimum(lam_re.astype(jnp.float32), S5_MAX_REAL)
    li = lam_im.astype(jnp.float32)
    dt = jnp.exp(log_dt.astype(jnp.float32))[:, None]
    mag = jnp.exp(lr * dt)
    ab_re = mag * jnp.cos(li * dt)
    ab_im = mag * jnp.sin(li * dt)
    den = lr * lr + li * li
    nr = ab_re - 1.0
    k_re = (nr * lr + ab_im * li) / den
    k_im = (ab_im * lr - nr * li) / den
    bb_re = k_re[..., None] * b_re - k_im[..., None] * b_im
    bb_im = k_re[..., None] * b_im + k_im[..., None] * b_re
    bu_re = jnp.einsum("blgh,gph->blgp", ug, bb_re)
    bu_im = jnp.einsum("blgh,gph->blgp", ug, bb_im)
    shape_a = (1, seq) + ab_re.shape
    a_re = jnp.broadcast_to(ab_re[None, None], shape_a)
    a_im = jnp.broadcast_to(ab_im[None, None], shape_a)
    _, _, h_re, h_im = lax.associative_scan(_complex_linear_combine,
                                            (a_re, a_im, bu_re, bu_im), axis=1)
    return jnp.einsum("blgp,ghp->blgh", h_re, c_re) - jnp.einsum("blgp,ghp->blgh", h_im, c_im)


def _s5_mixer(u, lam_re, lam_im, log_dt, b_re, b_im, c_re, c_im, d_skip, glu_w, glu_b):
    bsz, seq = u.shape[0], u.shape[1]
    ug = u.reshape(bsz, seq, S5_GROUPS, S5_GROUP_CH)
    y_f = _s5_scan(ug, lam_re[0], lam_im[0], log_dt[0], b_re, b_im, c_re, c_im)
    y_b = jnp.flip(_s5_scan(jnp.flip(ug, 1), lam_re[1], lam_im[1], log_dt[1],
                            b_re, b_im, c_re, c_im), 1)
    y = (y_f + y_b).reshape(bsz, seq, D_S5) + d_skip * u
    g = jax.nn.gelu(y)
    return g * jax.nn.sigmoid(g @ glu_w + glu_b)


def setup_inputs(seed: int = 0) -> dict:
    key = jax.random.key(seed)
    ks = iter(jax.random.split(key, 40))
    f32 = jnp.float32

    def nrm(shape, scale):
        return jax.random.normal(next(ks), shape, f32) * scale

    def gain(n):
        return 1.0 + nrm((DEPTH, n), 0.01)

    x = nrm((BATCH, SEQ, D_MODEL), 1.0)
    norm_mix_pre = gain(D_MODEL)
    w_in = nrm((DEPTH, D_MODEL, D_IN_PROJ), D_MODEL ** -0.5)
    conv_w = nrm((DEPTH, SSD_CONV, D_XBC), SSD_CONV ** -0.5)
    conv_b = nrm((DEPTH, D_XBC), 0.02)
    dt0 = jnp.exp(jax.random.uniform(next(ks), (DEPTH, 2, SSD_HEADS), f32,
                                     math.log(1e-3), math.log(1e-1)))
    ssd_dt_bias = dt0 + jnp.log(-jnp.expm1(-dt0))
    ssd_a_log = jnp.log(jax.random.uniform(next(ks), (DEPTH, 2, SSD_HEADS), f32, 1.0, 16.0))
    ssd_d = 1.0 + nrm((DEPTH, SSD_HEADS), 0.1)
    ssd_norm_w = gain(D_SSD)
    n = jnp.arange(S5_STATE, dtype=f32)
    s5_lambda_re = -0.5 + nrm((DEPTH, 2, S5_GROUPS, S5_STATE), 0.01)
    s5_lambda_im = math.pi * n + nrm((DEPTH, 2, S5_GROUPS, S5_STATE), 0.01)
    s5_log_dt = jax.random.uniform(next(ks), (DEPTH, 2, S5_GROUPS), f32,
                                   math.log(1e-3), math.log(1e-1))
    s5_b_re = nrm((DEPTH, S5_GROUPS, S5_STATE, S5_GROUP_CH), (2 * S5_GROUP_CH) ** -0.5)
    s5_b_im = nrm((DEPTH, S5_GROUPS, S5_STATE, S5_GROUP_CH), (2 * S5_GROUP_CH) ** -0.5)
    s5_c_re = nrm((DEPTH, S5_GROUPS, S5_GROUP_CH, S5_STATE), (2 * S5_STATE) ** -0.5)
    s5_c_im = nrm((DEPTH, S5_GROUPS, S5_GROUP_CH, S5_STATE), (2 * S5_STATE) ** -0.5)
    s5_d = nrm((DEPTH, D_S5), 1.0)
    s5_glu_w = nrm((DEPTH, D_S5, D_S5), D_S5 ** -0.5)
    s5_glu_b = nrm((DEPTH, D_S5), 0.02)
    w_out = nrm((DEPTH, MIX_WIDTH, D_MODEL), MIX_WIDTH ** -0.5)
    norm_mix_post = gain(D_MODEL)
    norm_ffn_pre = gain(D_MODEL)
    w_gate = nrm((DEPTH, D_MODEL, D_FF), D_MODEL ** -0.5)
    w_up = nrm((DEPTH, D_MODEL, D_FF), D_MODEL ** -0.5)
    w_down = nrm((DEPTH, D_FF, D_MODEL), D_FF ** -0.5)
    norm_ffn_post = gain(D_MODEL)
    return {"x": x, "norm_mix_pre": norm_mix_pre, "w_in": w_in, "conv_w": conv_w,
            "conv_b": conv_b, "ssd_dt_bias": ssd_dt_bias, "ssd_a_log": ssd_a_log,
            "ssd_d": ssd_d, "ssd_norm_w": ssd_norm_w, "s5_lambda_re": s5_lambda_re,
            "s5_lambda_im": s5_lambda_im, "s5_log_dt": s5_log_dt, "s5_b_re": s5_b_re,
            "s5_b_im": s5_b_im, "s5_c_re": s5_c_re, "s5_c_im": s5_c_im, "s5_d": s5_d,
            "s5_glu_w": s5_glu_w, "s5_glu_b": s5_glu_b, "w_out": w_out,
            "norm_mix_post": norm_mix_post, "norm_ffn_pre": norm_ffn_pre,
            "w_gate": w_gate, "w_up": w_up, "w_down": w_down,
            "norm_ffn_post": norm_ffn_post}


def reference(x, norm_mix_pre, w_in, conv_w, conv_b, ssd_dt_bias, ssd_a_log, ssd_d,
              ssd_norm_w, s5_lambda_re, s5_lambda_im, s5_log_dt, s5_b_re, s5_b_im,
              s5_c_re, s5_c_im, s5_d, s5_glu_w, s5_glu_b, w_out, norm_mix_post,
              norm_ffn_pre, w_gate, w_up, w_down, norm_ffn_post):
    x_dtype = x.dtype
    split_at = [D_SSD, D_SSD + D_XBC, D_SSD + D_XBC + 2 * SSD_HEADS]
    for layer in range(DEPTH):
        h = _rms_norm(x, norm_mix_pre[layer])
        proj = h @ w_in[layer]
        z, xbc, dt_raw, u = jnp.split(proj, split_at, axis=-1)
        y_ssd = _ssd_mixer(z, xbc, dt_raw, conv_w[layer], conv_b[layer], ssd_dt_bias[layer],
                           ssd_a_log[layer], ssd_d[layer], ssd_norm_w[layer])
        y_s5 = _s5_mixer(u, s5_lambda_re[layer], s5_lambda_im[layer], s5_log_dt[layer],
                         s5_b_re[layer], s5_b_im[layer], s5_c_re[layer], s5_c_im[layer],
                         s5_d[layer], s5_glu_w[layer], s5_glu_b[layer])
        mix = jnp.concatenate([y_ssd.astype(x_dtype), y_s5.astype(x_dtype)], axis=-1) @ w_out[layer]
        x = (x + _rms_norm(mix, norm_mix_post[layer])).astype(x_dtype)
        h = _rms_norm(x, norm_ffn_pre[layer])
        f = (jax.nn.silu(h @ w_gate[layer]) * (h @ w_up[layer])) @ w_down[layer]
        x = (x + _rms_norm(f, norm_ffn_post[layer])).astype(x_dtype)
    return x
```

```python
import functools

import jax
import jax.numpy as jnp
from jax import lax
from jax.experimental import pallas as pl
from jax.experimental.pallas import tpu as pltpu

F32 = jnp.float32
BF16 = jnp.bfloat16
HIGHEST = lax.Precision.HIGHEST

RMS_EPS = 1e-6
GATED_NORM_EPS = 1e-5
S5_MAX_REAL = -1e-4

SSD_HEAD_DIM = 64
SSD_GROUPS = 2
SSD_STATE = 128
SSD_CONV = 5
SSD_CHUNK = 128
CONV_ROWS = 128
CONV_HALO = 16

S5_GROUP_CH = 16
S5_STATE = 64
S5_CHUNK = 32

V7X_VMEM_LIMIT = 56 * 1024 * 1024


def _rms(x, w, eps):
    return x * lax.rsqrt(jnp.mean(x * x, axis=-1, keepdims=True) + eps) * w


def _silu(x):
    return x * (1.0 / (1.0 + jnp.exp(-x)))


def _sigmoid(x):
    return 1.0 / (1.0 + jnp.exp(-x))


def _softplus(x):
    return jnp.maximum(x, 0.0) + jnp.log1p(jnp.exp(-jnp.abs(x)))


def _gelu_tanh(x):
    c = 0.7978845608028654
    return 0.5 * x * (1.0 + jnp.tanh(c * (x + 0.044715 * (x * x * x))))


def _dot(a, b):
    return jnp.dot(a, b, preferred_element_type=F32)


def _dot_nt(a, b):
    return lax.dot_general(a, b, (((1,), (1,)), ((), ())), preferred_element_type=F32)


def _dot_hi(a, b):
    return jnp.dot(a, b, preferred_element_type=F32, precision=HIGHEST)


def _const_spec(shape):
    nd = len(shape)
    return pl.BlockSpec(shape, lambda *_: (0,) * nd)


def _in_proj_kernel(x_ref, g_ref, wz_ref, wxbc_ref, wdt_ref, wu_ref,
                    z_ref, xbc_ref, dt_ref, u_ref):
    h = _rms(x_ref[...], g_ref[...], RMS_EPS).astype(BF16)
    z_ref[...] = _dot(h, wz_ref[...]).astype(BF16)
    xbc_ref[...] = _dot(h, wxbc_ref[...]).astype(BF16)
    dt_ref[...] = _dot(h, wdt_ref[...])
    u_ref[...] = _dot(h, wu_ref[...]).astype(BF16)


def _in_proj(x2, gain, wz, wxbc, wdt, wu, tm):
    t, d = x2.shape
    row = lambda n: pl.BlockSpec((tm, n), lambda i: (i, 0))
    return pl.pallas_call(
        _in_proj_kernel,
        grid=(t // tm,),
        in_specs=[row(d), _const_spec(gain.shape), _const_spec(wz.shape),
                  _const_spec(wxbc.shape), _const_spec(wdt.shape), _const_spec(wu.shape)],
        out_specs=[row(wz.shape[1]), row(wxbc.shape[1]), row(wdt.shape[1]), row(wu.shape[1])],
        out_shape=[jax.ShapeDtypeStruct((t, wz.shape[1]), BF16),
                   jax.ShapeDtypeStruct((t, wxbc.shape[1]), BF16),
                   jax.ShapeDtypeStruct((t, wdt.shape[1]), F32),
                   jax.ShapeDtypeStruct((t, wu.shape[1]), BF16)],
        compiler_params=pltpu.CompilerParams(
            dimension_semantics=("arbitrary",), vmem_limit_bytes=V7X_VMEM_LIMIT),
        name="in_proj",
    )(x2, gain, wz, wxbc, wdt, wu)


def _ssd_kernel(z_ref, xbc_ref, dt_ref, dtt_ref, convw_ref, convb_ref, dtb_ref, dtbt_ref,
                alog_ref, alogt_ref, dskip_ref, normw_ref, expand_ref, out_ref,
                xs_s, b_s, c_s, dt_s, dtt_s, gin_s, st_s):
    seq = z_ref.shape[0]
    d_ssd = z_ref.shape[1]
    l = SSD_CHUNK
    nc = seq // l
    n = SSD_STATE
    heads = d_ssd // SSD_HEAD_DIM
    hg = heads // SSD_GROUPS
    gw = hg * SSD_HEAD_DIM
    gn = SSD_GROUPS * n

    r = CONV_ROWS
    halo = CONV_HALO
    pad = SSD_CONV // 2
    n_tiles = seq // r
    col_blocks = [(0, d_ssd, xs_s, 0), (d_ssd, d_ssd + gn, b_s, 0), (d_ssd + gn, d_ssd + 2 * gn, c_s, 0)]

    def conv_tile(i, carry):
        r0 = pl.multiple_of(i * r, r)
        lo = pl.multiple_of(jnp.maximum(r0 - halo, 0), halo)
        hi = pl.multiple_of(jnp.minimum(r0 + r, seq - halo), halo)
        keep_lo = (i > 0).astype(F32)
        keep_hi = (i < n_tiles - 1).astype(F32)
        for (c0, c1, dst, _) in col_blocks:
            for cb in range(c0, c1, 256):
                ce = min(cb + 256, c1)
                cur = xbc_ref[pl.ds(r0, r), cb:ce].astype(F32)
                prev = xbc_ref[pl.ds(lo, halo), cb:ce].astype(F32) * keep_lo
                nxt = xbc_ref[pl.ds(hi, halo), cb:ce].astype(F32) * keep_hi
                xx = jnp.concatenate([prev, cur, nxt], axis=0)
                acc = jnp.zeros((r, ce - cb), F32) + convb_ref[:, cb:ce]
                for k in range(SSD_CONV):
                    shift = (pad - k) % (r + 2 * halo)
                    tap = xx if shift == 0 else pltpu.roll(xx, shift, 0)
                    acc = acc + tap[halo:halo + r] * convw_ref[k:k + 1, cb:ce]
                dst[pl.ds(r0, r), cb - c0:ce - c0] = _silu(acc).astype(BF16)
        return carry

    lax.fori_loop(0, n_tiles, conv_tile, 0)

    dt_s[...] = _softplus(dt_ref[...] + dtb_ref[...])
    dtt_s[...] = _softplus(dtt_ref[...] + dtbt_ref[...][None])
    a_row = -jnp.exp(alog_ref[...])
    a_col = -jnp.exp(alogt_ref[...])

    ri = lax.broadcasted_iota(jnp.int32, (l, l), 0)
    ci = lax.broadcasted_iota(jnp.int32, (l, l), 1)
    lower = ri >= ci
    upper = ci >= ri
    tri_lo = lower.astype(F32)
    tri_up = upper.astype(F32)
    expand = expand_ref[...]

    def widen(v):
        return _dot(v.astype(BF16), expand)

    def chunk_scalars(c):
        r0 = pl.multiple_of(c * l, l)
        dt = dt_s[pl.ds(r0, l), :]
        dtt = dtt_s[c]
        a = dt * a_row
        at = dtt * a_col
        incl = _dot_hi(tri_lo, a)
        inclt = _dot_hi(at, tri_up)
        return r0, dt, dtt, a, at, incl, inclt

    st_s[...] = jnp.zeros_like(st_s)

    def bwd_chunk(i, carry):
        c = nc - 1 - i
        r0, dt, dtt, a, at, incl, inclt = chunk_scalars(c)
        for g in range(SSD_GROUPS):
            gin_s[c, g] = st_s[g].astype(BF16)
        excl_b = incl[:, heads:] - a[:, heads:]
        tot_b = incl[l - 1:l, heads:]
        w_b = jnp.exp(excl_b) * dt[:, heads:]
        xw = (xs_s[pl.ds(r0, l), :].astype(F32) * widen(w_b)).astype(BF16)
        dec = widen(jnp.broadcast_to(jnp.exp(tot_b), (8, heads)))[0:1]
        for g in range(SSD_GROUPS):
            bt = b_s[pl.ds(r0, l), g * n:(g + 1) * n].astype(F32).T.astype(BF16)
            sloc = _dot(bt, xw[:, g * gw:(g + 1) * gw])
            st_s[g] = st_s[g] * dec[:, g * gw:(g + 1) * gw] + sloc
        return carry

    lax.fori_loop(0, nc, bwd_chunk, 0)

    st_s[...] = jnp.zeros_like(st_s)
    lane = lax.broadcasted_iota(jnp.int32, (l, 2 * SSD_HEAD_DIM), 1)
    first_head = lane < SSD_HEAD_DIM

    def fwd_chunk(c, carry):
        r0, dt, dtt, a, at, incl, inclt = chunk_scalars(c)
        cum_f = incl[:, :heads]
        cum_ft = inclt[:heads]
        excl_b = incl[:, heads:] - a[:, heads:]
        excl_bt = inclt[heads:] - at[heads:]
        tot_f = incl[l - 1:l, :heads]
        tot_b = incl[l - 1:l, heads:]
        x_bf = xs_s[pl.ds(r0, l), :]
        x = x_bf.astype(F32)

        y_parts = []
        for g in range(SSD_GROUPS):
            cm = c_s[pl.ds(r0, l), g * n:(g + 1) * n]
            bm = b_s[pl.ds(r0, l), g * n:(g + 1) * n]
            scores = _dot_nt(cm, bm)
            for hp in range(hg // 2):
                ws = []
                for h in (g * hg + 2 * hp, g * hg + 2 * hp + 1):
                    mf = jnp.exp(jnp.where(lower, cum_f[:, h:h + 1] - cum_ft[h:h + 1, :], -jnp.inf))
                    mb = jnp.exp(jnp.where(upper, excl_bt[h:h + 1, :] - excl_b[:, h:h + 1], -jnp.inf))
                    m = mf * dtt[h:h + 1, :] + mb * dtt[heads + h:heads + h + 1, :]
                    ws.append((scores * m).astype(BF16))
                h0 = g * hg + 2 * hp
                xp = x_bf[:, h0 * SSD_HEAD_DIM:(h0 + 2) * SSD_HEAD_DIM]
                zero = jnp.zeros_like(xp)
                rhs = jnp.concatenate([jnp.where(first_head, xp, zero),
                                       jnp.where(first_head, zero, xp)], axis=0)
                y_parts.append(_dot(jnp.concatenate(ws, axis=1), rhs))
        y = jnp.concatenate(y_parts, axis=1)

        e_f = widen(jnp.exp(cum_f))
        e_b = widen(jnp.exp(tot_b - excl_b))
        w_f = jnp.exp(tot_f - cum_f) * dt[:, :heads]
        xw = (x * widen(w_f)).astype(BF16)
        dec = widen(jnp.broadcast_to(jnp.exp(tot_f), (8, heads)))[0:1]
        offs = []
        for g in range(SSD_GROUPS):
            cm = c_s[pl.ds(r0, l), g * n:(g + 1) * n]
            sl = slice(g * gw, (g + 1) * gw)
            off = (_dot(cm, st_s[g].astype(BF16)) * e_f[:, sl]
                   + _dot(cm, gin_s[c, g]) * e_b[:, sl])
            offs.append(off)
            bt = b_s[pl.ds(r0, l), g * n:(g + 1) * n].astype(F32).T.astype(BF16)
            st_s[g] = st_s[g] * dec[:, sl] + _dot(bt, xw[:, sl])
        y = y + jnp.concatenate(offs, axis=1) + x * dskip_ref[...]

        y = y * _silu(z_ref[pl.ds(r0, l), :].astype(F32))
        outs = []
        for g in range(SSD_GROUPS):
            sl = slice(g * gw, (g + 1) * gw)
            outs.append(_rms(y[:, sl], normw_ref[:, sl], GATED_NORM_EPS))
        out_ref[pl.ds(r0, l), :] = jnp.concatenate(outs, axis=1).astype(BF16)
        return carry

    lax.fori_loop(0, nc, fwd_chunk, 0)


def _ssd(z3, xbc3, dt3, dtt4, convw, convb, dtb, dtbt, alog, alogt, dskip, normw, expand):
    bsz, seq, d_ssd = z3.shape
    nc = seq // SSD_CHUNK
    gn = SSD_GROUPS * SSD_STATE
    gw = d_ssd // SSD_GROUPS
    per_b = lambda shape: pl.BlockSpec((None,) + shape, lambda b: (b,) + (0,) * len(shape))
    consts = [convw, convb, dtb, dtbt, alog, alogt, dskip, normw, expand]
    return pl.pallas_call(
        _ssd_kernel,
        grid=(bsz,),
        in_specs=[per_b(z3.shape[1:]), per_b(xbc3.shape[1:]), per_b(dt3.shape[1:]),
                  per_b(dtt4.shape[1:])] + [_const_spec(c.shape) for c in consts],
        out_specs=per_b((seq, d_ssd)),
        out_shape=jax.ShapeDtypeStruct((bsz, seq, d_ssd), BF16),
        scratch_shapes=[
            pltpu.VMEM((seq, d_ssd), BF16),
            pltpu.VMEM((seq, gn), BF16),
            pltpu.VMEM((seq, gn), BF16),
            pltpu.VMEM(dt3.shape[1:], F32),
            pltpu.VMEM(dtt4.shape[1:], F32),
            pltpu.VMEM((nc, SSD_GROUPS, SSD_STATE, gw), BF16),
            pltpu.VMEM((SSD_GROUPS, SSD_STATE, gw), F32),
        ],
        compiler_params=pltpu.CompilerParams(
            dimension_semantics=("arbitrary",), vmem_limit_bytes=V7X_VMEM_LIMIT),
        name="ssd",
    )(z3, xbc3, dt3, dtt4, *consts)


def _split_dot(a, sel):
    hi = a.astype(BF16)
    lo = (a - hi.astype(F32)).astype(BF16)
    return _dot(hi, sel) + _dot(lo, sel)


def _s5_gen_kernel(lre_c_ref, lim_c_ref, lre_r_ref, lim_r_ref, ldt_ref,
                   bt_re_ref, bt_im_ref, btile_re_ref, btile_im_ref, ctile_re_ref, ctile_im_ref,
                   tt_ref, rst_ref, om_ref, laml_ref):
    l = S5_CHUNK
    ch = S5_GROUP_CH
    p = S5_STATE
    w = l * ch
    w2 = 2 * w

    e_idx = lax.broadcasted_iota(jnp.int32, (128, w2), 0)
    slot = lax.shift_right_logical(lax.broadcasted_iota(jnp.int32, (128, w2), 1), 4)
    sel = (jnp.abs(slot - l) == e_idx).astype(BF16)
    slot_row = lax.shift_right_logical(lax.broadcasted_iota(jnp.int32, (1, w2), 1), 4)
    e_row = lax.broadcasted_iota(jnp.int32, (1, 128), 1).astype(F32)

    ctile_re = ctile_re_ref[...]
    ctile_im = ctile_im_ref[...]
    btile_re = btile_re_ref[...]
    btile_im = btile_im_ref[...]

    wc = jnp.zeros((ch, w2), F32)
    for d in range(2):
        dt = jnp.exp(ldt_ref[d])
        forms = []
        for lre_ref, lim_ref in ((lre_c_ref, lim_c_ref), (lre_r_ref, lim_r_ref)):
            lr = jnp.minimum(lre_ref[d], S5_MAX_REAL)
            li = lim_ref[d]
            mag = jnp.exp(lr * dt)
            ab_re = mag * jnp.cos(li * dt)
            ab_im = mag * jnp.sin(li * dt)
            den = lr * lr + li * li
            nr = ab_re - 1.0
            k_re = (nr * lr + ab_im * li) / den
            k_im = (ab_im * lr - nr * li) / den
            forms.append((lr, li, k_re, k_im))
        (lr_c, li_c, k_re_c, k_im_c), (lr_r, li_r, k_re_r, k_im_r) = forms

        mag = jnp.exp((lr_c * dt) * e_row)
        ang = (li_c * dt) * e_row
        pw_re = _split_dot(mag * jnp.cos(ang), sel)
        pw_im = _split_dot(mag * jnp.sin(ang), sel)

        own = (slot_row >= l) if d == 0 else ((slot_row >= 1) & (slot_row <= l))
        q_re = jnp.where(own, pw_re * ctile_re - pw_im * ctile_im, 0.0)
        q_im = jnp.where(own, pw_re * ctile_im + pw_im * ctile_re, 0.0)
        bbt_re = k_re_r * bt_re_ref[...] - k_im_r * bt_im_ref[...]
        bbt_im = k_re_r * bt_im_ref[...] + k_im_r * bt_re_ref[...]
        wc = wc + _dot_hi(bbt_re, q_re) - _dot_hi(bbt_im, q_im)

        bb_re = k_re_c * btile_re - k_im_c * btile_im
        bb_im = k_re_c * btile_im + k_im_c * btile_re
        s_in = slice(0, w) if d == 0 else slice(w, w2)
        s_out = slice(w, w2) if d == 0 else slice(0, w)
        pr, pi = pw_re[:, s_in], pw_im[:, s_in]
        rst_ref[(2 * d) * p:(2 * d + 1) * p, :] = (pr * bb_re - pi * bb_im).astype(BF16)
        rst_ref[(2 * d + 1) * p:(2 * d + 2) * p, :] = (pr * bb_im + pi * bb_re).astype(BF16)
        pr, pi = pw_re[:, s_out], pw_im[:, s_out]
        cr, ci = ctile_re[:, :w], ctile_im[:, :w]
        om_ref[(2 * d) * p:(2 * d + 1) * p, :] = (pr * cr - pi * ci).astype(BF16)
        om_ref[(2 * d + 1) * p:(2 * d + 2) * p, :] = (-(pr * ci + pi * cr)).astype(BF16)
        mag_l = jnp.exp(lr_r * dt * float(l))
        laml_ref[2 * d:2 * d + 1, :] = mag_l * jnp.cos(li_r * dt * float(l))
        laml_ref[2 * d + 1:2 * d + 2, :] = mag_l * jnp.sin(li_r * dt * float(l))

    for jp in range(l):
        off = (l - jp) * ch
        rolled = pltpu.roll(wc, (w2 - off) % w2, 1)
        tt_ref[jp * ch:(jp + 1) * ch, :] = rolled[:, :w].astype(BF16)


def _s5_gen(lre_c, lim_c, lre_r, lim_r, ldt, bt_re, bt_im, btile_re, btile_im, ctile_re, ctile_im):
    groups = lre_c.shape[0]
    w = S5_CHUNK * S5_GROUP_CH
    p = S5_STATE
    ins = [lre_c, lim_c, lre_r, lim_r, ldt, bt_re, bt_im, btile_re, btile_im, ctile_re, ctile_im]
    per_g = lambda shape: pl.BlockSpec((None,) + shape, lambda g: (g,) + (0,) * len(shape))
    return pl.pallas_call(
        _s5_gen_kernel,
        grid=(groups,),
        in_specs=[per_g(a.shape[1:]) for a in ins],
        out_specs=[per_g((w, w)), per_g((4 * p, w)), per_g((4 * p, w)), per_g((4, p))],
        out_shape=[jax.ShapeDtypeStruct((groups, w, w), BF16),
                   jax.ShapeDtypeStruct((groups, 4 * p, w), BF16),
                   jax.ShapeDtypeStruct((groups, 4 * p, w), BF16),
                   jax.ShapeDtypeStruct((groups, 4, p), F32)],
        compiler_params=pltpu.CompilerParams(
            dimension_semantics=("arbitrary",), vmem_limit_bytes=V7X_VMEM_LIMIT),
        name="s5_gen",
    )(*ins)


def _s5_scan_kernel(u_ref, tt_ref, rst_ref, om_ref, laml_ref, y_ref, sloc_s, sin_s, *, bsz):
    p = S5_STATE
    rows = u_ref.shape[0]
    nc = rows // bsz
    u = u_ref[...]
    sloc_s[...] = _dot_nt(u, rst_ref[...])

    lam = laml_ref[...]
    lf_re, lf_im, lb_re, lb_im = lam[0:1], lam[1:2], lam[2:3], lam[3:4]
    zero = jnp.zeros((bsz, p), F32)

    def step(i, carry):
        f_re, f_im, b_re, b_im = carry
        rf = pl.multiple_of(i * bsz, bsz)
        rb = pl.multiple_of((nc - 1 - i) * bsz, bsz)
        sin_s[pl.ds(rf, bsz), 0:p] = f_re
        sin_s[pl.ds(rf, bsz), p:2 * p] = f_im
        sin_s[pl.ds(rb, bsz), 2 * p:3 * p] = b_re
        sin_s[pl.ds(rb, bsz), 3 * p:4 * p] = b_im
        lf = sloc_s[pl.ds(rf, bsz), :]
        lb = sloc_s[pl.ds(rb, bsz), :]
        nf_re = lf_re * f_re - lf_im * f_im + lf[:, 0:p]
        nf_im = lf_re * f_im + lf_im * f_re + lf[:, p:2 * p]
        nb_re = lb_re * b_re - lb_im * b_im + lb[:, 2 * p:3 * p]
        nb_im = lb_re * b_im + lb_im * b_re + lb[:, 3 * p:4 * p]
        return nf_re, nf_im, nb_re, nb_im

    lax.fori_loop(0, nc, step, (zero, zero, zero, zero))
    y = _dot(u, tt_ref[...]) + _dot(sin_s[...].astype(BF16), om_ref[...])
    y_ref[...] = y.astype(BF16)


def _s5_scan(u_g, tt, rst, om, laml, bsz):
    groups, rows, w = u_g.shape
    p = S5_STATE
    per_g = lambda shape: pl.BlockSpec((None,) + shape, lambda g: (g,) + (0,) * len(shape))
    return pl.pallas_call(
        functools.partial(_s5_scan_kernel, bsz=bsz),
        grid=(groups,),
        in_specs=[per_g((rows, w)), per_g(tt.shape[1:]), per_g(rst.shape[1:]),
                  per_g(om.shape[1:]), per_g(laml.shape[1:])],
        out_specs=per_g((rows, w)),
        out_shape=jax.ShapeDtypeStruct((groups, rows, w), BF16),
        scratch_shapes=[pltpu.VMEM((rows, 4 * p), F32), pltpu.VMEM((rows, 4 * p), F32)],
        compiler_params=pltpu.CompilerParams(
            dimension_semantics=("arbitrary",), vmem_limit_bytes=V7X_VMEM_LIMIT),
        name="s5_scan",
    )(u_g, tt, rst, om, laml)


def _mix_out_kernel(x_ref, yssd_ref, y5_ref, u_ref, d5_ref, gluw_ref, glub_ref,
                    wout_a_ref, wout_b_ref, gain_ref, o_ref):
    y = y5_ref[...].astype(F32) + d5_ref[...] * u_ref[...].astype(F32)
    g = _gelu_tanh(y)
    gate = _sigmoid(_dot(g.astype(BF16), gluw_ref[...]) + glub_ref[...])
    y_s5 = (g * gate).astype(BF16)
    mix = _dot(yssd_ref[...], wout_a_ref[...]) + _dot(y_s5, wout_b_ref[...])
    o_ref[...] = x_ref[...] + _rms(mix, gain_ref[...], RMS_EPS)


def _mix_out(x2, yssd, y5, u, d5, gluw, glub, wout_a, wout_b, gain, tm):
    t, d = x2.shape
    row = lambda n: pl.BlockSpec((tm, n), lambda i: (i, 0))
    consts = [d5, gluw, glub, wout_a, wout_b, gain]
    return pl.pallas_call(
        _mix_out_kernel,
        grid=(t // tm,),
        in_specs=[row(d), row(yssd.shape[1]), row(y5.shape[1]), row(u.shape[1])]
                 + [_const_spec(c.shape) for c in consts],
        out_specs=row(d),
        out_shape=jax.ShapeDtypeStruct((t, d), F32),
        compiler_params=pltpu.CompilerParams(
            dimension_semantics=("arbitrary",), vmem_limit_bytes=V7X_VMEM_LIMIT),
        name="mix_out",
    )(x2, yssd, y5, u, *consts)


def _ffn_kernel(x_ref, gpre_ref, wg_ref, wu_ref, wd_ref, gpost_ref, o_ref):
    x = x_ref[...]
    h = _rms(x, gpre_ref[...], RMS_EPS).astype(BF16)
    a = _silu(_dot(h, wg_ref[...])) * _dot(h, wu_ref[...])
    f = _dot(a.astype(BF16), wd_ref[...])
    o_ref[...] = x + _rms(f, gpost_ref[...], RMS_EPS)


def _ffn(x2, gpre, wg, wu, wd, gpost, tm):
    t, d = x2.shape
    row = pl.BlockSpec((tm, d), lambda i: (i, 0))
    consts = [gpre, wg, wu, wd, gpost]
    return pl.pallas_call(
        _ffn_kernel,
        grid=(t // tm,),
        in_specs=[row] + [_const_spec(c.shape) for c in consts],
        out_specs=row,
        out_shape=jax.ShapeDtypeStruct((t, d), F32),
        compiler_params=pltpu.CompilerParams(
            dimension_semantics=("arbitrary",), vmem_limit_bytes=V7X_VMEM_LIMIT),
        name="ffn",
    )(x2, *consts)


def _layer(x, norm_mix_pre, w_in, conv_w, conv_b, ssd_dt_bias, ssd_a_log, ssd_d, ssd_norm_w,
           s5_lambda_re, s5_lambda_im, s5_log_dt, s5_b_re, s5_b_im, s5_c_re, s5_c_im, s5_d,
           s5_glu_w, s5_glu_b, w_out, norm_mix_post, norm_ffn_pre, w_gate, w_up, w_down,
           norm_ffn_post):
    bsz, seq, d_model = x.shape
    t = bsz * seq
    heads = ssd_d.shape[0]
    d_ssd = heads * SSD_HEAD_DIM
    d_xbc = conv_w.shape[1]
    d_s5 = s5_d.shape[0]
    groups = d_s5 // S5_GROUP_CH
    s0, s1, s2 = d_ssd, d_ssd + d_xbc, d_ssd + d_xbc + 2 * heads
    row = lambda v: v.reshape(1, -1).astype(F32)

    x2 = x.reshape(t, d_model)
    wb = w_in.astype(BF16)
    z, xbc, dt_raw, u = _in_proj(x2, row(norm_mix_pre), wb[:, :s0], wb[:, s0:s1],
                                 wb[:, s1:s2], wb[:, s2:], tm=512)

    nc = seq // SSD_CHUNK
    dt3 = dt_raw.reshape(bsz, seq, 2 * heads)
    dtt4 = dt3.reshape(bsz, nc, SSD_CHUNK, 2 * heads).transpose(0, 1, 3, 2)
    expand = jnp.repeat(jnp.eye(heads, dtype=BF16), SSD_HEAD_DIM, axis=1)
    y_ssd = _ssd(z.reshape(bsz, seq, d_ssd), xbc.reshape(bsz, seq, d_xbc), dt3, dtt4,
                 conv_w.astype(F32), row(conv_b), row(ssd_dt_bias),
                 ssd_dt_bias.reshape(-1, 1).astype(F32), row(ssd_a_log),
                 ssd_a_log.reshape(-1, 1).astype(F32),
                 row(jnp.repeat(ssd_d, SSD_HEAD_DIM)), row(ssd_norm_w), expand)

    l5 = S5_CHUNK
    nc5 = seq // l5
    lam_c = lambda v: jnp.swapaxes(v, 0, 1)[..., None].astype(F32)
    lam_r = lambda v: jnp.swapaxes(v, 0, 1)[:, :, None, :].astype(F32)
    ldt = jnp.swapaxes(s5_log_dt, 0, 1)[..., None, None].astype(F32)
    bt = lambda v: jnp.swapaxes(v, 1, 2).astype(F32)
    btile = lambda v: jnp.tile(v.astype(F32), (1, 1, l5))
    ctile = lambda v: jnp.tile(jnp.swapaxes(v, 1, 2).astype(F32), (1, 1, 2 * l5))
    tt, rst, om, laml = _s5_gen(lam_c(s5_lambda_re), lam_c(s5_lambda_im), lam_r(s5_lambda_re),
                                lam_r(s5_lambda_im), ldt, bt(s5_b_re), bt(s5_b_im),
                                btile(s5_b_re), btile(s5_b_im), ctile(s5_c_re), ctile(s5_c_im))
    u_g = (u.reshape(bsz, nc5, l5, groups, S5_GROUP_CH).transpose(3, 1, 0, 2, 4)
           .reshape(groups, nc5 * bsz, l5 * S5_GROUP_CH))
    y5_g = _s5_scan(u_g, tt, rst, om, laml, bsz)
    y5 = (y5_g.reshape(groups, nc5, bsz, l5, S5_GROUP_CH).transpose(2, 1, 3, 0, 4)
          .reshape(t, d_s5))

    wo = w_out.astype(BF16)
    x1 = _mix_out(x2, y_ssd.reshape(t, d_ssd), y5, u, row(s5_d), s5_glu_w.astype(BF16),
                  row(s5_glu_b), wo[:d_ssd], wo[d_ssd:], row(norm_mix_post), tm=512)
    x2o = _ffn(x1, row(norm_ffn_pre), w_gate.astype(BF16), w_up.astype(BF16),
               w_down.astype(BF16), row(norm_ffn_post), tm=256)
    return x2o.reshape(bsz, seq, d_model)


def kernel(x, norm_mix_pre, w_in, conv_w, conv_b, ssd_dt_bias, ssd_a_log, ssd_d, ssd_norm_w,
           s5_lambda_re, s5_lambda_im, s5_log_dt, s5_b_re, s5_b_im, s5_c_re, s5_c_im, s5_d,
           s5_glu_w, s5_glu_b, w_out, norm_mix_post, norm_ffn_pre, w_gate, w_up, w_down,
           norm_ffn_post):
    params = (norm_mix_pre, w_in, conv_w, conv_b, ssd_dt_bias, ssd_a_log, ssd_d, ssd_norm_w,
              s5_lambda_re, s5_lambda_im, s5_log_dt, s5_b_re, s5_b_im, s5_c_re, s5_c_im, s5_d,
              s5_glu_w, s5_glu_b, w_out, norm_mix_post, norm_ffn_pre, w_gate, w_up, w_down,
              norm_ffn_post)
    for layer in range(norm_mix_pre.shape[0]):
        x = _layer(x, *[p[layer] for p in params])
    return x
```
